```python
import jax
import jax.numpy as jnp
from jax import lax
import numpy as np

D_MODEL = 1024
BATCH = 4
SEQ = 4096
DEPTH = 2

CHUNK = 64
QBLOCK = 128
ROPE_THETA = 500000.0
NORM_EPS = 1e-6

SB_HEADS = 8
SB_HEAD_DIM = 64
SGU_GROUPS = 8
SGU_GROUP_DIM = 64
SGU_CHUNK = 128
AB_SPLITS = (SB_HEADS * SB_HEAD_DIM, SB_HEADS * SB_HEAD_DIM, SB_HEADS * SB_HEAD_DIM,
             SGU_GROUPS * SGU_GROUP_DIM, SGU_GROUPS * SGU_GROUP_DIM)
AB_COLS = sum(AB_SPLITS)
AB_MIX = SB_HEADS * SB_HEAD_DIM + SGU_GROUPS * SGU_GROUP_DIM

MLA_HEADS = 8
MLA_Q_RANK = 256
MLA_KV_RANK = 128
MLA_ROPE_DIM = 32
MLA_NOPE_DIM = 64
MLA_V_DIM = 64
DSA_HEADS = 8
DSA_HEAD_DIM = 64
DSA_ROT_DIM = DSA_HEAD_DIM // 4
IDX_HEADS = 8
IDX_HEAD_DIM = 32
IDX_ROT_DIM = IDX_HEAD_DIM // 4
INDEX_TOPK = 256
CD_SPLITS = (MLA_Q_RANK, MLA_KV_RANK, MLA_ROPE_DIM,
             DSA_HEADS * DSA_HEAD_DIM, DSA_HEADS * DSA_HEAD_DIM, DSA_HEADS * DSA_HEAD_DIM,
             IDX_HEADS * IDX_HEAD_DIM, IDX_HEAD_DIM, IDX_HEADS)
CD_COLS = sum(CD_SPLITS)
CD_MIX = MLA_HEADS * MLA_V_DIM + DSA_HEADS * DSA_HEAD_DIM

MOE_GROUPS = 4
EXPERTS_PER_GROUP = 4
N_EXPERTS = MOE_GROUPS * EXPERTS_PER_GROUP
EXPERT_TOPK = 2
D_EXPERT = 512

N_EVEN_LAYERS = (DEPTH + 1) // 2
N_ODD_LAYERS = DEPTH // 2

kernel_name = 'hybrid_stickbreak_sgu_mla_dsa_hmoe'


def rms_norm(x, g):
    x32 = x.astype(jnp.float32)
    y = x32 * lax.rsqrt(jnp.mean(x32 * x32, axis=-1, keepdims=True) + NORM_EPS)
    return (y * g.astype(jnp.float32)).astype(x.dtype)


def rotary(x, positions, rot_dim):
    half = rot_dim // 2
    inv_freq = jnp.power(jnp.float32(ROPE_THETA), -jnp.arange(half, dtype=jnp.float32) * (2.0 / rot_dim))
    ang = positions.astype(jnp.float32)[:, :, None, None] * inv_freq
    cos, sin = jnp.cos(ang), jnp.sin(ang)
    x32 = x.astype(jnp.float32)
    x1, x2 = x32[..., :half], x32[..., half:rot_dim]
    out = jnp.concatenate([x1 * cos - x2 * sin, x2 * cos + x1 * sin, x32[..., rot_dim:]], axis=-1)
    return out.astype(x.dtype)


def split_cols(t, sizes):
    return jnp.split(t, [int(c) for c in np.cumsum(sizes)[:-1]], axis=-1)


def to_blocks(t):
    b, s = t.shape[0], t.shape[1]
    return jnp.swapaxes(t.reshape((b, s // QBLOCK, QBLOCK) + t.shape[2:]), 0, 1)


def from_blocks(t):
    nb, b = t.shape[0], t.shape[1]
    t = jnp.swapaxes(t, 0, 1)
    return t.reshape((b, nb * QBLOCK) + t.shape[3:])


def stick_breaking_attention(q, k, v):
    s_len, d = q.shape[1], q.shape[-1]
    scale = d ** -0.5
    key_pos = jnp.arange(s_len)

    def one_block(args):
        q_blk, blk = args
        q_pos = blk * QBLOCK + jnp.arange(QBLOCK)
        strict = key_pos[None, :] < q_pos[:, None]
        z = jnp.einsum('bqhd,bshd->bhqs', q_blk, k).astype(jnp.float32) * scale
        log_not = jnp.where(strict, -jax.nn.softplus(z), 0.0)
        later = lax.cumsum(log_not, axis=3, reverse=True) - log_not
        w = jnp.where(strict, jnp.exp(jax.nn.log_sigmoid(z) + later), 0.0)
        return jnp.einsum('bhqs,bshd->bqhd', w.astype(v.dtype), v)

    return from_blocks(lax.map(one_block, (to_blocks(q), jnp.arange(s_len // QBLOCK))))


def spatial_gating(u, z, norm_g, w_s, b_s):
    b, s_len, g, dg = z.shape
    z = rms_norm(z, norm_g)
    zc = z.reshape(b, s_len // SGU_CHUNK, SGU_CHUNK, g, dg)
    i = jnp.arange(SGU_CHUNK) // CHUNK
    visible = i[None, :] <= i[:, None]
    w = jnp.where(visible[None], w_s, 0.0).astype(z.dtype)
    mixed = jnp.einsum('gij,bnjgd->bnigd', w, zc) + jnp.swapaxes(b_s, 0, 1)[None, None, :, :, None]
    return u * mixed.reshape(b, s_len, g, dg)


def chunk_causal_attention(q, k, v):
    s_len = q.shape[1]
    scale = q.shape[-1] ** -0.5
    key_chunk = jnp.arange(s_len) // CHUNK

    def one_block(args):
        q_blk, blk = args
        q_chunk = (blk * QBLOCK + jnp.arange(QBLOCK)) // CHUNK
        visible = key_chunk[None, :] <= q_chunk[:, None]
        sc = jnp.einsum('bqhd,bshd->bhqs', q_blk, k).astype(jnp.float32) * scale
        p = jax.nn.softmax(jnp.where(visible, sc, -jnp.inf), axis=-1)
        return jnp.einsum('bhqs,bshd->bqhd', p.astype(v.dtype), v)

    return from_blocks(lax.map(one_block, (to_blocks(q), jnp.arange(s_len // QBLOCK))))


def indexed_sparse_attention(q, k, v, iq, ik, iw, topk):
    s_len = q.shape[1]
    scale = q.shape[-1] ** -0.5
    idx_scale = iq.shape[-1] ** -0.5
    key_chunk = jnp.arange(s_len) // CHUNK

    def one_block(args):
        q_blk, iq_blk, iw_blk, blk = args
        q_chunk = (blk * QBLOCK + jnp.arange(QBLOCK)) // CHUNK
        admissible = key_chunk[None, :] <= q_chunk[:, None]
        logits = jnp.einsum('bqhe,bse->bqhs', iq_blk, ik).astype(jnp.float32) * idx_scale
        score = jnp.einsum('bqh,bqhs->bqs', iw_blk.astype(jnp.float32), jax.nn.relu(logits))
        score = jnp.where(admissible[None], score, -jnp.inf)
        top_score, top_idx = lax.top_k(score, topk)
        valid = top_score > -jnp.inf
        k_sel = jax.vmap(lambda kb, ib: kb[ib])(k, top_idx)
        v_sel = jax.vmap(lambda vb, ib: vb[ib])(v, top_idx)
        sc = jnp.einsum('bqhd,bqkhd->bhqk', q_blk, k_sel).astype(jnp.float32) * scale
        p = jax.nn.softmax(jnp.where(valid[:, None], sc, -jnp.inf), axis=-1)
        return jnp.einsum('bhqk,bqkhd->bqhd', p.astype(v.dtype), v_sel)

    blocks = (to_blocks(q), to_blocks(iq), to_blocks(iw), jnp.arange(s_len // QBLOCK))
    return from_blocks(lax.map(one_block, blocks))


def even_mixer(h, w_in, sb_q_g, sb_k_g, sgu_g, w_s, b_s, w_out):
    b, s_len, _ = h.shape
    q, k, v, u, z = split_cols(h @ w_in, AB_SPLITS)
    hshape = (b, s_len, SB_HEADS, SB_HEAD_DIM)
    gshape = (b, s_len, SGU_GROUPS, SGU_GROUP_DIM)
    a_out = stick_breaking_attention(rms_norm(q.reshape(hshape), sb_q_g),
                                     rms_norm(k.reshape(hshape), sb_k_g), v.reshape(hshape))
    b_out = spatial_gating(jax.nn.gelu(u).reshape(gshape), jax.nn.gelu(z).reshape(gshape), sgu_g, w_s, b_s)
    mixed = jnp.concatenate([a_out.reshape(b, s_len, -1), b_out.reshape(b, s_len, -1)], axis=-1)
    return mixed @ w_out


def odd_mixer(h, positions, w_in, q_lat_g, kv_lat_g, w_uq, w_ukv, mla_q_g, mla_kn_g, mla_kr_g,
              dsa_q_g, dsa_k_g, w_out, topk):
    b, s_len, _ = h.shape
    c_q, c_kv, k_pe, q_d, k_d, v_d, iq, ik, iw = split_cols(h @ w_in, CD_SPLITS)
    q_c = (rms_norm(c_q, q_lat_g) @ w_uq).reshape(b, s_len, MLA_HEADS, MLA_ROPE_DIM + MLA_NOPE_DIM)
    kv_c = (rms_norm(c_kv, kv_lat_g) @ w_ukv).reshape(b, s_len, MLA_HEADS, MLA_NOPE_DIM + MLA_V_DIM)
    k_nope, v_c = kv_c[..., :MLA_NOPE_DIM], kv_c[..., MLA_NOPE_DIM:]
    q_c = rotary(rms_norm(q_c, mla_q_g), positions, MLA_ROPE_DIM)
    k_rope = rotary(rms_norm(k_pe, mla_kr_g)[:, :, None, :], positions, MLA_ROPE_DIM)
    k_c = jnp.concatenate([jnp.broadcast_to(k_rope, (b, s_len, MLA_HEADS, MLA_ROPE_DIM)),
                           rms_norm(k_nope, mla_kn_g)], axis=-1)
    c_out = chunk_causal_attention(q_c, k_c, v_c)
    dshape = (b, s_len, DSA_HEADS, DSA_HEAD_DIM)
    q_d = rotary(rms_norm(q_d.reshape(dshape), dsa_q_g), positions, DSA_ROT_DIM)
    k_d = rotary(rms_norm(k_d.reshape(dshape), dsa_k_g), positions, DSA_ROT_DIM)
    iq = rotary(iq.reshape(b, s_len, IDX_HEADS, IDX_HEAD_DIM), positions, IDX_ROT_DIM)
    ik = rotary(ik[:, :, None, :], positions, IDX_ROT_DIM)[:, :, 0]
    iw = iw * (IDX_HEADS ** -0.5)
    d_out = indexed_sparse_attention(q_d, k_d, v_d.reshape(dshape), iq, ik, iw, topk)
    mixed = jnp.concatenate([c_out.reshape(b, s_len, -1), d_out.reshape(b, s_len, -1)], axis=-1)
    return mixed @ w_out


def hierarchical_moe(h, w_group, b_group, w_expert, b_expert, w_gate, w_up, w_down):
    b, s_len, d = h.shape
    t = h.reshape(b * s_len, d)
    f32 = jnp.float32
    group_probs = jax.nn.softmax((t @ w_group).astype(f32) + b_group.astype(f32), axis=-1)
    p_group, g_idx = lax.top_k(group_probs, 1)
    expert_logits = jnp.einsum('td,gde->tge', t, w_expert).astype(f32) + b_expert.astype(f32)
    in_group = jnp.einsum('tge,tg->te', expert_logits, jax.nn.one_hot(g_idx[:, 0], MOE_GROUPS, dtype=f32))
    p_in, e_idx = lax.top_k(jax.nn.softmax(in_group, axis=-1), EXPERT_TOPK)
    combine = p_group * p_in / jnp.sum(p_in, axis=-1, keepdims=True)
    expert_id = g_idx * EXPERTS_PER_GROUP + e_idx
    gates = jnp.einsum('tk,tke->te', combine, jax.nn.one_hot(expert_id, N_EXPERTS, dtype=f32)).astype(h.dtype)
    y = jnp.zeros_like(t)
    for e in range(N_EXPERTS):
        hid = jax.nn.silu(t @ w_gate[e]) * (t @ w_up[e])
        y = y + (hid @ w_down[e]) * gates[:, e:e + 1]
    return y.reshape(b, s_len, d)


def setup_inputs(seed: int = 0) -> dict:
    key = jax.random.key(seed)
    keys = list(jax.random.split(key, 40))
    f32 = jnp.float32

    def normal(shape, scale):
        return jax.random.normal(keys.pop(), shape, f32) * scale

    def gain(shape):
        return 1.0 + 0.02 * jax.random.normal(keys.pop(), shape, f32)

    ne, no = N_EVEN_LAYERS, N_ODD_LAYERS
    x = normal((BATCH, SEQ, D_MODEL), 1.0)
    offset = jax.random.randint(keys.pop(), (BATCH, 1), 0, 256) * CHUNK
    positions = (offset + jnp.arange(SEQ)[None, :]).astype(jnp.int32)
    return {
        'x': x,
        'positions': positions,
        'ab_norm_g': gain((ne, D_MODEL)),
        'ab_w_in': normal((ne, D_MODEL, AB_COLS), D_MODEL ** -0.5),
        'sb_q_norm_g': gain((ne, SB_HEAD_DIM)),
        'sb_k_norm_g': gain((ne, SB_HEAD_DIM)),
        'sgu_norm_g': gain((ne, SGU_GROUPS, SGU_GROUP_DIM)),
        'sgu_w_s': normal((ne, SGU_GROUPS, SGU_CHUNK, SGU_CHUNK), SGU_CHUNK ** -0.5),
        'sgu_b_s': gain((ne, SGU_GROUPS, SGU_CHUNK)),
        'ab_w_out': normal((ne, AB_MIX, D_MODEL), AB_MIX ** -0.5),
        'cd_norm_g': gain((no, D_MODEL)),
        'cd_w_in': normal((no, D_MODEL, CD_COLS), D_MODEL ** -0.5),
        'mla_q_latent_norm_g': gain((no, MLA_Q_RANK)),
        'mla_kv_latent_norm_g': gain((no, MLA_KV_RANK)),
        'mla_w_uq': normal((no, MLA_Q_RANK, MLA_HEADS * (MLA_ROPE_DIM + MLA_NOPE_DIM)), MLA_Q_RANK ** -0.5),
        'mla_w_ukv': normal((no, MLA_KV_RANK, MLA_HEADS * (MLA_NOPE_DIM + MLA_V_DIM)), MLA_KV_RANK ** -0.5),
        'mla_q_norm_g': gain((no, MLA_ROPE_DIM + MLA_NOPE_DIM)),
        'mla_k_nope_norm_g': gain((no, MLA_NOPE_DIM)),
        'mla_k_rope_norm_g': gain((no, MLA_ROPE_DIM)),
        'dsa_q_norm_g': gain((no, DSA_HEAD_DIM)),
        'dsa_k_norm_g': gain((no, DSA_HEAD_DIM)),
        'cd_w_out': normal((no, CD_MIX, D_MODEL), CD_MIX ** -0.5),
        'ffn_norm_g': gain((DEPTH, D_MODEL)),
        'router_group_w': normal((DEPTH, D_MODEL, MOE_GROUPS), D_MODEL ** -0.5),
        'router_group_b': normal((DEPTH, MOE_GROUPS), 0.01),
        'router_expert_w': normal((DEPTH, MOE_GROUPS, D_MODEL, EXPERTS_PER_GROUP), D_MODEL ** -0.5),
        'router_expert_b': normal((DEPTH, MOE_GROUPS, EXPERTS_PER_GROUP), 0.01),
        'expert_w_gate': normal((DEPTH, N_EXPERTS, D_MODEL, D_EXPERT), D_MODEL ** -0.5),
        'expert_w_up': normal((DEPTH, N_EXPERTS, D_MODEL, D_EXPERT), D_MODEL ** -0.5),
        'expert_w_down': normal((DEPTH, N_EXPERTS, D_EXPERT, D_MODEL), D_EXPERT ** -0.5),
    }


def reference(x, positions, ab_norm_g, ab_w_in, sb_q_norm_g, sb_k_norm_g, sgu_norm_g, sgu_w_s, sgu_b_s,
              ab_w_out, cd_norm_g, cd_w_in, mla_q_latent_norm_g, mla_kv_latent_norm_g, mla_w_uq, mla_w_ukv,
              mla_q_norm_g, mla_k_nope_norm_g, mla_k_rope_norm_g, dsa_q_norm_g, dsa_k_norm_g, cd_w_out,
              ffn_norm_g, router_group_w, router_group_b, router_expert_w, router_expert_b,
              expert_w_gate, expert_w_up, expert_w_down):
    topk = min(INDEX_TOPK, x.shape[1] // 4)
    h = x
    for layer in range(DEPTH):
        i = layer // 2
        if layer % 2 == 0:
            h = h + even_mixer(rms_norm(h, ab_norm_g[i]), ab_w_in[i], sb_q_norm_g[i], sb_k_norm_g[i],
                               sgu_norm_g[i], sgu_w_s[i], sgu_b_s[i], ab_w_out[i])
        else:
            h = h + odd_mixer(rms_norm(h, cd_norm_g[i]), positions, cd_w_in[i], mla_q_latent_norm_g[i],
                              mla_kv_latent_norm_g[i], mla_w_uq[i], mla_w_ukv[i], mla_q_norm_g[i],
                              mla_k_nope_norm_g[i], mla_k_rope_norm_g[i], dsa_q_norm_g[i], dsa_k_norm_g[i],
                              cd_w_out[i], topk)
        h = h + hierarchical_moe(rms_norm(h, ffn_norm_g[layer]), router_group_w[layer], router_group_b[layer],
                                 router_expert_w[layer], router_expert_b[layer], expert_w_gate[layer],
                                 expert_w_up[layer], expert_w_down[layer])
    return h
```

```python
import functools

import numpy as np
import jax
import jax.numpy as jnp
from jax import lax
from jax.experimental import pallas as pl
from jax.experimental.pallas import tpu as pltpu

F32 = jnp.float32
BF16 = jnp.bfloat16

LANES = 128
VMEM_LIMIT = 48 * 1024 * 1024

D_MODEL = 1024
CHUNK = 64
ROPE_THETA = 500000.0
NORM_EPS = 1e-6

SB_HEADS = 8
HEAD_DIM = 64
SGU_CHUNK = 128

MLA_HEADS = 8
MLA_Q_RANK = 256
MLA_KV_RANK = 128
MLA_ROPE_DIM = 32
MLA_NOPE_DIM = 64
DSA_ROT_DIM = 16
IDX_HEADS = 8
IDX_HEAD_DIM = 32
IDX_ROT_DIM = 8
INDEX_TOPK = 256

MOE_GROUPS = 4
EXPERTS_PER_GROUP = 4
N_EXPERTS = 16
D_EXPERT = 512

ATT_BLOCK = 256
ROW_TILE = 256
MOE_TILE = 512
MASK_VALUE = -1e30
INT_MIN = -2147483648


def _params(*semantics):
    return pltpu.CompilerParams(dimension_semantics=semantics, vmem_limit_bytes=VMEM_LIMIT)


def _lane_iota(shape=(1, LANES)):
    return lax.broadcasted_iota(jnp.int32, shape, len(shape) - 1)


def _rms(x, g, n):
    ms = jnp.sum(x * x, axis=-1, keepdims=True) * (1.0 / n)
    return x * lax.rsqrt(ms + NORM_EPS) * g


def _pair_rms(x, g):
    in_a = _lane_iota() < HEAD_DIM
    ss = x * x
    sa = jnp.sum(jnp.where(in_a, ss, 0.0), axis=-1, keepdims=True)
    sb = jnp.sum(jnp.where(in_a, 0.0, ss), axis=-1, keepdims=True)
    ms = jnp.where(in_a, sa, sb) * (1.0 / HEAD_DIM)
    return x * lax.rsqrt(ms + NORM_EPS) * g


def _rotate(x, tab_ref, half):
    up = pltpu.roll(x, LANES - half, 1)
    down = pltpu.roll(x, half, 1)
    return x * tab_ref[0] + up * tab_ref[1] + down * tab_ref[2]


def _dot_t(a, b):
    return lax.dot_general(a, b, (((1,), (1,)), ((), ())), preferred_element_type=F32)


def _norm_matmul_kernel(x_ref, g_ref, w_ref, o_ref):
    xb = _rms(x_ref[...], g_ref[...], D_MODEL).astype(BF16)
    n_blocks = o_ref.shape[0]
    step = 4
    for c in range(0, n_blocks, step):
        nb = min(step, n_blocks - c)
        acc = jnp.dot(xb, w_ref[:, c * LANES:(c + nb) * LANES], preferred_element_type=F32)
        for i in range(nb):
            o_ref[c + i] = acc[:, i * LANES:(i + 1) * LANES]


def _norm_matmul(h, g, w):
    t, d = h.shape
    n = w.shape[1]
    nb = n // LANES
    return pl.pallas_call(
        _norm_matmul_kernel,
        grid=(t // ROW_TILE,),
        in_specs=[pl.BlockSpec((ROW_TILE, d), lambda i: (i, 0)),
                  pl.BlockSpec((1, d), lambda i: (0, 0)),
                  pl.BlockSpec((d, n), lambda i: (0, 0))],
        out_specs=pl.BlockSpec((nb, ROW_TILE, LANES), lambda i: (0, i, 0)),
        out_shape=jax.ShapeDtypeStruct((nb, t, LANES), F32),
        compiler_params=_params("parallel"),
        name="norm_matmul",
    )(h, g.reshape(1, d), w)


def _out_proj_kernel(h_ref, a_ref, b_ref, w1_ref, w2_ref, o_ref):
    acc = jnp.dot(a_ref[...], w1_ref[...], preferred_element_type=F32)
    acc = acc + jnp.dot(b_ref[...], w2_ref[...], preferred_element_type=F32)
    o_ref[...] = h_ref[...] + acc


def _out_proj(h, a, b, w):
    t, d = h.shape
    ka, kb = a.shape[1], b.shape[1]
    return pl.pallas_call(
        _out_proj_kernel,
        grid=(t // ROW_TILE,),
        in_specs=[pl.BlockSpec((ROW_TILE, d), lambda i: (i, 0)),
                  pl.BlockSpec((ROW_TILE, ka), lambda i: (i, 0)),
                  pl.BlockSpec((ROW_TILE, kb), lambda i: (i, 0)),
                  pl.BlockSpec((ka, d), lambda i: (0, 0)),
                  pl.BlockSpec((kb, d), lambda i: (0, 0))],
        out_specs=pl.BlockSpec((ROW_TILE, d), lambda i: (i, 0)),
        out_shape=jax.ShapeDtypeStruct((t, d), F32),
        compiler_params=_params("parallel"),
        name="out_proj",
    )(h, a, b, w[:ka].astype(BF16), w[ka:].astype(BF16))


def _sb_kernel(q_ref, k_ref, v_ref, gq_ref, gk_ref, o_ref, kn_ref, vb_ref, *, seq):
    tq = ATT_BLOCK
    i = pl.program_id(2)
    in_a = _lane_iota() < HEAD_DIM

    @pl.when(i == 0)
    def _():
        def prep(r, carry):
            rows = pl.ds(pl.multiple_of(r * tq, tq), tq)
            kn_ref[rows, :] = _pair_rms(k_ref[0, rows, :], gk_ref[...]).astype(BF16)
            vb_ref[rows, :] = v_ref[0, rows, :].astype(BF16)
            return carry
        lax.fori_loop(0, seq // tq, prep, 0)

    qn = _pair_rms(q_ref[0], gq_ref[...]) * (HEAD_DIM ** -0.5)
    q_heads = (jnp.where(in_a, qn, 0.0).astype(BF16), jnp.where(in_a, 0.0, qn).astype(BF16))

    row = lax.broadcasted_iota(jnp.int32, (tq, tq), 0)
    col = lax.broadcasted_iota(jnp.int32, (tq, tq), 1)
    strict = col < row
    suffix = jnp.where(row > col, 1.0, 0.0).astype(BF16)

    def tile(qh, kj, vj, c, acc, mask):
        z = _dot_t(qh, kj)
        sp = jnp.maximum(z, 0.0) + jnp.log1p(jnp.exp(-jnp.abs(z)))
        log_not = -sp
        if mask is not None:
            log_not = jnp.where(mask, log_not, 0.0)
        later = jnp.dot(log_not.astype(BF16), suffix, preferred_element_type=F32) + c
        w = jnp.exp((z - sp) + later)
        if mask is not None:
            w = jnp.where(mask, w, 0.0)
        c = c + jnp.sum(log_not, axis=-1, keepdims=True)
        acc = acc + jnp.dot(w.astype(BF16), vj, preferred_element_type=F32)
        return c, acc

    diag = pl.ds(pl.multiple_of(i * tq, tq), tq)
    k_diag, v_diag = kn_ref[diag, :], vb_ref[diag, :]
    outs = []
    for qh in q_heads:
        c0 = jnp.zeros((tq, 1), F32)
        acc0 = jnp.zeros((tq, LANES), F32)
        c0, acc0 = tile(qh, k_diag, v_diag, c0, acc0, strict)

        def body(n, carry, qh=qh):
            rows = pl.ds(pl.multiple_of((i - 1 - n) * tq, tq), tq)
            return tile(qh, kn_ref[rows, :], vb_ref[rows, :], carry[0], carry[1], None)

        _, acc = lax.fori_loop(0, i, body, (c0, acc0))
        outs.append(acc)
    o_ref[...] = jnp.where(in_a, outs[0], outs[1]).astype(o_ref.dtype)


def _sb_attention(proj, gq, gk, batch, seq):
    t = batch * seq
    nq = seq // ATT_BLOCK
    pairs = SB_HEADS // 2
    return pl.pallas_call(
        functools.partial(_sb_kernel, seq=seq),
        grid=(batch, pairs, nq),
        in_specs=[pl.BlockSpec((1, ATT_BLOCK, LANES), lambda b, p, i: (p, b * nq + i, 0)),
                  pl.BlockSpec((1, seq, LANES), lambda b, p, i: (pairs + p, b, 0)),
                  pl.BlockSpec((1, seq, LANES), lambda b, p, i: (2 * pairs + p, b, 0)),
                  pl.BlockSpec((1, LANES), lambda b, p, i: (0, 0)),
                  pl.BlockSpec((1, LANES), lambda b, p, i: (0, 0))],
        out_specs=pl.BlockSpec((ATT_BLOCK, LANES), lambda b, p, i: (b * nq + i, p)),
        out_shape=jax.ShapeDtypeStruct((t, pairs * LANES), BF16),
        scratch_shapes=[pltpu.VMEM((seq, LANES), BF16), pltpu.VMEM((seq, LANES), BF16)],
        compiler_params=_params("parallel", "parallel", "arbitrary"),
        name="stick_breaking",
    )(proj, proj, proj, jnp.tile(gq, 2).reshape(1, LANES), jnp.tile(gk, 2).reshape(1, LANES))


def _sgu_kernel(u_ref, z_ref, g_ref, w_ref, b_ref, o_ref):
    in_a = _lane_iota() < HEAD_DIM
    zn = _pair_rms(jax.nn.gelu(z_ref[0]), g_ref[0]).astype(BF16)
    u = jax.nn.gelu(u_ref[0])
    row = lax.broadcasted_iota(jnp.int32, (SGU_CHUNK, SGU_CHUNK), 0) // CHUNK
    col = lax.broadcasted_iota(jnp.int32, (SGU_CHUNK, SGU_CHUNK), 1) // CHUNK
    visible = col <= row
    wa = jnp.where(visible, w_ref[0], 0.0).astype(BF16)
    wb = jnp.where(visible, w_ref[1], 0.0).astype(BF16)
    for n in range(u.shape[0] // SGU_CHUNK):
        rows = slice(n * SGU_CHUNK, (n + 1) * SGU_CHUNK)
        zc = zn[rows]
        mixed = jnp.where(in_a, jnp.dot(wa, zc, preferred_element_type=F32),
                          jnp.dot(wb, zc, preferred_element_type=F32)) + b_ref[0]
        o_ref[rows, :] = (u[rows] * mixed).astype(o_ref.dtype)


def _spatial_gating(proj, sgu_g, w_s, b_s):
    t = proj.shape[1]
    pairs = w_s.shape[0] // 2
    g = sgu_g.reshape(pairs, 1, LANES)
    bias = jnp.repeat(b_s.reshape(pairs, 2, SGU_CHUNK), HEAD_DIM, axis=1).transpose(0, 2, 1)
    return pl.pallas_call(
        _sgu_kernel,
        grid=(t // ROW_TILE, pairs),
        in_specs=[pl.BlockSpec((1, ROW_TILE, LANES), lambda i, p: (12 + p, i, 0)),
                  pl.BlockSpec((1, ROW_TILE, LANES), lambda i, p: (16 + p, i, 0)),
                  pl.BlockSpec((1, 1, LANES), lambda i, p: (p, 0, 0)),
                  pl.BlockSpec((2, SGU_CHUNK, SGU_CHUNK), lambda i, p: (p, 0, 0)),
                  pl.BlockSpec((1, SGU_CHUNK, LANES), lambda i, p: (p, 0, 0))],
        out_specs=pl.BlockSpec((ROW_TILE, LANES), lambda i, p: (i, p)),
        out_shape=jax.ShapeDtypeStruct((t, pairs * LANES), BF16),
        compiler_params=_params("parallel", "parallel"),
        name="spatial_gating",
    )(proj, proj, g, w_s, bias)


def _moe_kernel(h_ref, g_ref, wr_ref, br_ref, wgu_ref, wd_ref, o_ref, xn_ref, gates_ref):
    e = pl.program_id(1)
    lane = _lane_iota()
    first_expert_lane = MOE_GROUPS

    @pl.when(e == 0)
    def _():
        x = h_ref[...]
        xn = _rms(x, g_ref[...], D_MODEL)
        xn_ref[...] = xn.astype(BF16)
        logits = jnp.dot(xn, wr_ref[...], preferred_element_type=F32,
                         precision=lax.Precision.HIGHEST) + br_ref[...]
        neg = -jnp.inf
        is_group = lane < MOE_GROUPS
        mg = jnp.max(jnp.where(is_group, logits, neg), axis=-1, keepdims=True)
        sg = jnp.sum(jnp.exp(jnp.where(is_group, logits - mg, neg)), axis=-1, keepdims=True)
        p_group = 1.0 / sg
        g_idx = jnp.min(jnp.where(is_group & (logits == mg), lane, LANES), axis=-1, keepdims=True)
        lo = first_expert_lane + EXPERTS_PER_GROUP * g_idx
        in_group = (lane >= lo) & (lane < lo + EXPERTS_PER_GROUP)
        m1 = jnp.max(jnp.where(in_group, logits, neg), axis=-1, keepdims=True)
        e1 = jnp.min(jnp.where(in_group & (logits == m1), lane, LANES), axis=-1, keepdims=True)
        rest = in_group & (lane != e1)
        m2 = jnp.max(jnp.where(rest, logits, neg), axis=-1, keepdims=True)
        e2 = jnp.min(jnp.where(rest & (logits == m2), lane, LANES), axis=-1, keepdims=True)
        r = jnp.exp(m2 - m1)
        c1 = p_group / (1.0 + r)
        gates_ref[...] = jnp.where(lane == e1, c1, 0.0) + jnp.where(lane == e2, c1 * r, 0.0)
        o_ref[...] = x

    gate = jnp.sum(jnp.where(lane == e + first_expert_lane, gates_ref[...], 0.0), axis=-1, keepdims=True)
    gu = jnp.dot(xn_ref[...], wgu_ref[0], preferred_element_type=F32)
    hid = jax.nn.silu(gu[:, :D_EXPERT]) * gu[:, D_EXPERT:] * gate
    o_ref[...] += jnp.dot(hid.astype(BF16), wd_ref[0], preferred_element_type=F32)


def _moe(h, g, w_group, b_group, w_expert, b_expert, w_gate, w_up, w_down):
    t, d = h.shape
    pad = LANES - MOE_GROUPS - N_EXPERTS
    wr = jnp.concatenate([w_group, w_expert.transpose(1, 0, 2).reshape(d, N_EXPERTS),
                          jnp.zeros((d, pad), F32)], axis=1)
    br = jnp.concatenate([b_group, b_expert.reshape(N_EXPERTS), jnp.zeros((pad,), F32)]).reshape(1, LANES)
    wgu = jnp.concatenate([w_gate, w_up], axis=2).astype(BF16)
    return pl.pallas_call(
        _moe_kernel,
        grid=(t // MOE_TILE, N_EXPERTS),
        in_specs=[pl.BlockSpec((MOE_TILE, d), lambda i, e: (i, 0)),
                  pl.BlockSpec((1, d), lambda i, e: (0, 0)),
                  pl.BlockSpec((d, LANES), lambda i, e: (0, 0)),
                  pl.BlockSpec((1, LANES), lambda i, e: (0, 0)),
                  pl.BlockSpec((1, d, 2 * D_EXPERT), lambda i, e: (e, 0, 0)),
                  pl.BlockSpec((1, D_EXPERT, d), lambda i, e: (e, 0, 0))],
        out_specs=pl.BlockSpec((MOE_TILE, d), lambda i, e: (i, 0)),
        out_shape=jax.ShapeDtypeStruct((t, d), F32),
        scratch_shapes=[pltpu.VMEM((MOE_TILE, d), BF16), pltpu.VMEM((MOE_TILE, LANES), F32)],
        compiler_params=_params("parallel", "arbitrary"),
        name="moe",
    )(h, g.reshape(1, d), wr, br, wgu, w_down.astype(BF16))


CD_CQ, CD_CKV, CD_QD, CD_KD, CD_VD, CD_IQ, CD_IK, CD_MISC = 0, 2, 3, 7, 11, 15, 17, 18
CD_BLOCKS = 19
KPE_LANE = 64


def _cd_prep_kernel(p_ref, rot_mla_ref, rot_dsa_ref, rot_idx_ref, gql_ref, gkvl_ref, wuq_ref, wk_ref,
                    wv_ref, gq_ref, gkn_ref, gkr_ref, gdq_ref, gdk_ref,
                    qc_ref, kc_ref, vc_ref, qd_ref, kd_ref, vd_ref, iq_ref, ik_ref, iw_ref):
    lane = _lane_iota()
    head_w = MLA_ROPE_DIM + MLA_NOPE_DIM
    cq = jnp.concatenate([p_ref[CD_CQ], p_ref[CD_CQ + 1]], axis=1)
    cqn = _rms(cq, gql_ref[...], MLA_Q_RANK).astype(BF16)
    ckvn = _rms(p_ref[CD_CKV], gkvl_ref[...], MLA_KV_RANK).astype(BF16)
    misc = p_ref[CD_MISC]
    kpe = jnp.where((lane >= KPE_LANE) & (lane < KPE_LANE + MLA_ROPE_DIM), misc, 0.0)
    k_rope = _rotate(_rms(kpe, gkr_ref[...], MLA_ROPE_DIM), rot_mla_ref, MLA_ROPE_DIM // 2)
    for h in range(MLA_HEADS):
        cols = slice(h * LANES, (h + 1) * LANES)
        q = jnp.dot(cqn, wuq_ref[:, cols], preferred_element_type=F32)
        q = _rotate(_rms(q, gq_ref[...], head_w), rot_mla_ref, MLA_ROPE_DIM // 2)
        qc_ref[h] = (q * (head_w ** -0.5)).astype(BF16)
        kn = jnp.dot(ckvn, wk_ref[:, cols], preferred_element_type=F32)
        kc_ref[h] = (_rms(kn, gkn_ref[...], MLA_NOPE_DIM) + k_rope).astype(BF16)
    for p in range(MLA_HEADS // 2):
        cols = slice(p * LANES, (p + 1) * LANES)
        vc_ref[p] = jnp.dot(ckvn, wv_ref[:, cols], preferred_element_type=F32).astype(BF16)
    for p in range(SB_HEADS // 2):
        q = _rotate(_pair_rms(p_ref[CD_QD + p], gdq_ref[...]), rot_dsa_ref, DSA_ROT_DIM // 2)
        qd_ref[p] = (q * (HEAD_DIM ** -0.5)).astype(BF16)
        k = _rotate(_pair_rms(p_ref[CD_KD + p], gdk_ref[...]), rot_dsa_ref, DSA_ROT_DIM // 2)
        kd_ref[p] = k.astype(BF16)
        vd_ref[p] = p_ref[CD_VD + p].astype(BF16)
    for p in range(2):
        iq = _rotate(p_ref[CD_IQ + p], rot_idx_ref, IDX_ROT_DIM // 2)
        iq_ref[p] = (iq * (IDX_HEAD_DIM ** -0.5)).astype(BF16)
    ik_ref[...] = _rotate(p_ref[CD_IK], rot_idx_ref, IDX_ROT_DIM // 2).astype(BF16)
    iw_ref[...] = misc * (IDX_HEADS ** -0.5)


def _rot_tables(positions, rot_dim, group_w, offset):
    half = rot_dim // 2
    inv_freq = jnp.power(jnp.float32(ROPE_THETA), -jnp.arange(half, dtype=F32) * (2.0 / rot_dim))
    ang = positions.reshape(-1).astype(F32)[:, None] * inv_freq
    t = ang.shape[0]
    cos = jnp.concatenate([jnp.cos(ang), jnp.ones((t, 1), F32)], axis=1)
    sin = jnp.concatenate([jnp.sin(ang), jnp.zeros((t, 1), F32)], axis=1)
    p = np.arange(LANES) % group_w - offset
    first = (p >= 0) & (p < half)
    second = (p >= half) & (p < rot_dim)
    idx_c = np.where(first, p, np.where(second, p - half, half))
    idx_1 = np.where(first, p, half)
    idx_2 = np.where(second, p - half, half)
    return jnp.stack([cos[:, idx_c], -sin[:, idx_1], sin[:, idx_2]])


def _cd_prep(proj, positions, q_lat_g, kv_lat_g, w_uq, w_ukv, mla_q_g, mla_kn_g, mla_kr_g, dsa_q_g, dsa_k_g):
    t = proj.shape[1]
    head_w = MLA_ROPE_DIM + MLA_NOPE_DIM
    zpad = LANES - head_w
    wuq = w_uq.reshape(MLA_Q_RANK, MLA_HEADS, head_w)
    wuq = jnp.concatenate([wuq[..., MLA_ROPE_DIM:], wuq[..., :MLA_ROPE_DIM],
                           jnp.zeros((MLA_Q_RANK, MLA_HEADS, zpad), F32)], axis=-1)
    wuq = wuq.reshape(MLA_Q_RANK, MLA_HEADS * LANES).astype(BF16)
    wukv = w_ukv.reshape(MLA_KV_RANK, MLA_HEADS, MLA_NOPE_DIM + HEAD_DIM)
    wk = jnp.concatenate([wukv[..., :MLA_NOPE_DIM], jnp.zeros((MLA_KV_RANK, MLA_HEADS, LANES - MLA_NOPE_DIM), F32)],
                         axis=-1).reshape(MLA_KV_RANK, MLA_HEADS * LANES).astype(BF16)
    wv = wukv[..., MLA_NOPE_DIM:].reshape(MLA_KV_RANK, MLA_HEADS * HEAD_DIM).astype(BF16)
    gq = jnp.concatenate([mla_q_g[MLA_ROPE_DIM:], mla_q_g[:MLA_ROPE_DIM], jnp.zeros((zpad,), F32)])
    gkn = jnp.concatenate([mla_kn_g, jnp.zeros((LANES - MLA_NOPE_DIM,), F32)])
    gkr = jnp.concatenate([jnp.zeros((KPE_LANE,), F32), mla_kr_g,
                           jnp.zeros((LANES - KPE_LANE - MLA_ROPE_DIM,), F32)])
    rot_mla = _rot_tables(positions, MLA_ROPE_DIM, LANES, MLA_NOPE_DIM)
    rot_dsa = _rot_tables(positions, DSA_ROT_DIM, HEAD_DIM, 0)
    rot_idx = _rot_tables(positions, IDX_ROT_DIM, IDX_HEAD_DIM, 0)

    tm = ROW_TILE
    row = lambda i: (i, 0)
    const2 = lambda i: (0, 0)
    blk3 = lambda i: (0, i, 0)
    vec = pl.BlockSpec((1, LANES), const2)
    bf_blocks = lambda n: jax.ShapeDtypeStruct((n, t, LANES), BF16)
    blk_out = lambda n: pl.BlockSpec((n, tm, LANES), blk3)
    return pl.pallas_call(
        _cd_prep_kernel,
        grid=(t // tm,),
        in_specs=[pl.BlockSpec((CD_BLOCKS, tm, LANES), blk3),
                  pl.BlockSpec((3, tm, LANES), blk3),
                  pl.BlockSpec((3, tm, LANES), blk3),
                  pl.BlockSpec((3, tm, LANES), blk3),
                  pl.BlockSpec((1, MLA_Q_RANK), const2),
                  vec,
                  pl.BlockSpec(wuq.shape, const2),
                  pl.BlockSpec(wk.shape, const2),
                  pl.BlockSpec(wv.shape, const2),
                  vec, vec, vec, vec, vec],
        out_specs=[blk_out(8), blk_out(8), blk_out(4), blk_out(4), blk_out(4), blk_out(4), blk_out(2),
                   pl.BlockSpec((tm, LANES), row), pl.BlockSpec((tm, LANES), row)],
        out_shape=[bf_blocks(8), bf_blocks(8), bf_blocks(4), bf_blocks(4), bf_blocks(4), bf_blocks(4),
                   bf_blocks(2), jax.ShapeDtypeStruct((t, LANES), BF16), jax.ShapeDtypeStruct((t, LANES), F32)],
        compiler_params=_params("parallel"),
        name="cd_prep",
    )(proj, rot_mla, rot_dsa, rot_idx, q_lat_g.reshape(1, -1), kv_lat_g.reshape(1, -1), wuq, wk, wv,
      gq.reshape(1, -1), gkn.reshape(1, -1), gkr.reshape(1, -1),
      jnp.tile(dsa_q_g, 2).reshape(1, -1), jnp.tile(dsa_k_g, 2).reshape(1, -1))


def _softmax_step(s, vj, m, l, acc):
    m_new = jnp.maximum(m, jnp.max(s, axis=-1, keepdims=True))
    alpha = jnp.exp(m - m_new)
    p = jnp.exp(s - m_new)
    l = alpha * l + jnp.sum(p, axis=-1, keepdims=True)
    acc = alpha * acc + jnp.dot(p.astype(BF16), vj, preferred_element_type=F32)
    return m_new, l, acc


def _softmax_init(tq):
    return (jnp.full((tq, 1), MASK_VALUE, F32), jnp.zeros((tq, 1), F32), jnp.zeros((tq, LANES), F32))


def _mla_kernel(q_ref, k_ref, v_ref, o_ref):
    tq = ATT_BLOCK
    i = pl.program_id(2)
    in_a = _lane_iota() < HEAD_DIM
    row = lax.broadcasted_iota(jnp.int32, (tq, tq), 0) // CHUNK
    col = lax.broadcasted_iota(jnp.int32, (tq, tq), 1) // CHUNK
    visible = col <= row
    diag = pl.ds(pl.multiple_of(i * tq, tq), tq)
    outs = []
    for h in range(2):
        q = q_ref[h]

        def body(j, carry, q=q, h=h):
            rows = pl.ds(pl.multiple_of(j * tq, tq), tq)
            return _softmax_step(_dot_t(q, k_ref[h, rows, :]), v_ref[0, rows, :], *carry)

        carry = lax.fori_loop(0, i, body, _softmax_init(tq))
        s = jnp.where(visible, _dot_t(q, k_ref[h, diag, :]), MASK_VALUE)
        _, l, acc = _softmax_step(s, v_ref[0, diag, :], *carry)
        outs.append(acc / l)
    o_ref[...] = jnp.where(in_a, outs[0], outs[1]).astype(o_ref.dtype)


def _mla_attention(qc, kc, vc, batch, seq):
    t = batch * seq
    nq = seq // ATT_BLOCK
    pairs = MLA_HEADS // 2
    return pl.pallas_call(
        _mla_kernel,
        grid=(batch, pairs, nq),
        in_specs=[pl.BlockSpec((2, ATT_BLOCK, LANES), lambda b, p, i: (p, b * nq + i, 0)),
                  pl.BlockSpec((2, seq, LANES), lambda b, p, i: (p, b, 0)),
                  pl.BlockSpec((1, seq, LANES), lambda b, p, i: (p, b, 0))],
        out_specs=pl.BlockSpec((ATT_BLOCK, LANES), lambda b, p, i: (b * nq + i, p)),
        out_shape=jax.ShapeDtypeStruct((t, pairs * LANES), BF16),
        compiler_params=_params("parallel", "parallel", "arbitrary"),
        name="latent_attention",
    )(qc, kc, vc)


def _dsa_kernel(qd_ref, kd_ref, vd_ref, iq_ref, ik_ref, iw_ref, o_ref, key_ref, bias_ref, *, topk):
    tq = ATT_BLOCK
    i = pl.program_id(1)
    n_blk = i + 1
    lane = _lane_iota()
    in_a = lane < HEAD_DIM
    row = lax.broadcasted_iota(jnp.int32, (tq, tq), 0)
    col = lax.broadcasted_iota(jnp.int32, (tq, tq), 1)
    admissible = (col // CHUNK) <= (row // CHUNK)

    iw = iw_ref[...]
    iw_cols = [iw[:, h:h + 1] for h in range(IDX_HEADS)]
    per_blk = LANES // IDX_HEAD_DIM
    iq_heads = [jnp.where(lane // IDX_HEAD_DIM == h % per_blk, iq_ref[h // per_blk], 0.0).astype(BF16)
                for h in range(IDX_HEADS)]

    def score_keys(j):
        ik = ik_ref[pl.ds(pl.multiple_of(j * tq, tq), tq), :]
        score = jnp.zeros((tq, tq), F32)
        for h in range(IDX_HEADS):
            score = score + iw_cols[h] * jnp.maximum(_dot_t(iq_heads[h], ik), 0.0)
        bits = pltpu.bitcast(score, jnp.int32)
        return bits ^ ((bits >> 31) & 0x7FFFFFFF)

    def score_body(j, carry):
        key_ref[j] = score_keys(j)
        return carry

    lax.fori_loop(0, i, score_body, 0)
    key_ref[i] = jnp.where(admissible, score_keys(i), INT_MIN)

    def bit_body(b, prefix):
        cand = prefix | lax.shift_left(jnp.int32(1), 31 - b)
        cand_key = cand ^ INT_MIN

        def count_body(j, cnt):
            ge = jnp.where(key_ref[j] >= cand_key, 1, 0)
            return cnt + ge[:, :LANES] + ge[:, LANES:]

        cnt = lax.fori_loop(0, n_blk, count_body, jnp.zeros((tq, LANES), jnp.int32))
        return jnp.where(jnp.sum(cnt, axis=-1, keepdims=True) >= topk, cand, prefix)

    prefix = lax.cond(n_blk * tq > topk,
                      lambda: lax.fori_loop(0, 32, bit_body, jnp.zeros((tq, 1), jnp.int32)),
                      lambda: jnp.zeros((tq, 1), jnp.int32))
    kth = prefix ^ INT_MIN

    def gt_body(j, cnt):
        gt = jnp.where(key_ref[j] > kth, 1, 0)
        return cnt + gt[:, :LANES] + gt[:, LANES:]

    n_gt = jnp.sum(lax.fori_loop(0, n_blk, gt_body, jnp.zeros((tq, LANES), jnp.int32)), axis=-1, keepdims=True)
    n_ties = (topk - n_gt).astype(F32)

    before = jnp.where(row < col, 1.0, 0.0).astype(BF16)

    def bias_body(j, seen):
        key = key_ref[j]
        tie = key == kth
        tie_f = jnp.where(tie, 1.0, 0.0)
        rank = jnp.dot(tie_f.astype(BF16), before, preferred_element_type=F32) + seen
        chosen = ((key > kth) | (tie & (rank < n_ties))) & (key != INT_MIN)
        bias_ref[j] = jnp.where(chosen, 0.0, MASK_VALUE)
        return seen + jnp.sum(tie_f, axis=-1, keepdims=True)

    lax.fori_loop(0, n_blk, bias_body, jnp.zeros((tq, 1), F32))

    for p in range(SB_HEADS // 2):
        q = qd_ref[p]
        qa = jnp.where(in_a, q, 0.0).astype(BF16)
        qb = jnp.where(in_a, 0.0, q).astype(BF16)

        def body(j, carry, p=p, qa=qa, qb=qb):
            rows = pl.ds(pl.multiple_of(j * tq, tq), tq)
            kj, vj, bias = kd_ref[p, rows, :], vd_ref[p, rows, :], bias_ref[j]
            ca = _softmax_step(_dot_t(qa, kj) + bias, vj, *carry[0])
            cb = _softmax_step(_dot_t(qb, kj) + bias, vj, *carry[1])
            return ca, cb

        (_, la, acca), (_, lb, accb) = lax.fori_loop(0, n_blk, body, (_softmax_init(tq), _softmax_init(tq)))
        o_ref[:, p * LANES:(p + 1) * LANES] = jnp.where(in_a, acca / la, accb / lb).astype(o_ref.dtype)


def _dsa_attention(qd, kd, vd, iq, ik, iw, batch, seq, topk):
    t = batch * seq
    nq = seq // ATT_BLOCK
    pairs = SB_HEADS // 2
    return pl.pallas_call(
        functools.partial(_dsa_kernel, topk=topk),
        grid=(batch, nq),
        in_specs=[pl.BlockSpec((pairs, ATT_BLOCK, LANES), lambda b, i: (0, b * nq + i, 0)),
                  pl.BlockSpec((pairs, seq, LANES), lambda b, i: (0, b, 0)),
                  pl.BlockSpec((pairs, seq, LANES), lambda b, i: (0, b, 0)),
                  pl.BlockSpec((2, ATT_BLOCK, LANES), lambda b, i: (0, b * nq + i, 0)),
                  pl.BlockSpec((seq, LANES), lambda b, i: (b, 0)),
                  pl.BlockSpec((ATT_BLOCK, LANES), lambda b, i: (b * nq + i, 0))],
        out_specs=pl.BlockSpec((ATT_BLOCK, pairs * LANES), lambda b, i: (b * nq + i, 0)),
        out_shape=jax.ShapeDtypeStruct((t, pairs * LANES), BF16),
        scratch_shapes=[pltpu.VMEM((nq, ATT_BLOCK, ATT_BLOCK), jnp.int32),
                        pltpu.VMEM((nq, ATT_BLOCK, ATT_BLOCK), F32)],
        compiler_params=_params("parallel", "arbitrary"),
        name="sparse_attention",
    )(qd, kd, vd, iq, ik, iw)


def _even_mixer(h, batch, seq, norm_g, w_in, sb_q_g, sb_k_g, sgu_g, w_s, b_s, w_out):
    proj = _norm_matmul(h, norm_g, w_in.astype(BF16))
    a_out = _sb_attention(proj, sb_q_g, sb_k_g, batch, seq)
    b_out = _spatial_gating(proj, sgu_g, w_s, b_s)
    return _out_proj(h, a_out, b_out, w_out)


def _arrange_cd_w_in(w_in):
    d = w_in.shape[0]
    sizes = (MLA_Q_RANK, MLA_KV_RANK, MLA_ROPE_DIM, SB_HEADS * HEAD_DIM, SB_HEADS * HEAD_DIM,
             SB_HEADS * HEAD_DIM, IDX_HEADS * IDX_HEAD_DIM, IDX_HEAD_DIM, IDX_HEADS)
    c_q, c_kv, k_pe, q_d, k_d, v_d, iq, ik, iw = jnp.split(w_in, [int(c) for c in np.cumsum(sizes)[:-1]], axis=1)
    zeros = lambda n: jnp.zeros((d, n), w_in.dtype)
    misc = [iw, zeros(KPE_LANE - IDX_HEADS), k_pe, zeros(LANES - KPE_LANE - MLA_ROPE_DIM)]
    return jnp.concatenate([c_q, c_kv, q_d, k_d, v_d, iq, ik, ik, ik, ik] + misc, axis=1)


def _odd_mixer(h, positions, batch, seq, norm_g, w_in, q_lat_g, kv_lat_g, w_uq, w_ukv, mla_q_g, mla_kn_g,
               mla_kr_g, dsa_q_g, dsa_k_g, w_out, topk):
    proj = _norm_matmul(h, norm_g, _arrange_cd_w_in(w_in).astype(BF16))
    qc, kc, vc, qd, kd, vd, iq, ik, iw = _cd_prep(proj, positions, q_lat_g, kv_lat_g, w_uq, w_ukv, mla_q_g,
                                                  mla_kn_g, mla_kr_g, dsa_q_g, dsa_k_g)
    c_out = _mla_attention(qc, kc, vc, batch, seq)
    d_out = _dsa_attention(qd, kd, vd, iq, ik, iw, batch, seq, topk)
    return _out_proj(h, c_out, d_out, w_out)


def kernel(x, positions, ab_norm_g, ab_w_in, sb_q_norm_g, sb_k_norm_g, sgu_norm_g, sgu_w_s, sgu_b_s, ab_w_out,
           cd_norm_g, cd_w_in, mla_q_latent_norm_g, mla_kv_latent_norm_g, mla_w_uq, mla_w_ukv, mla_q_norm_g,
           mla_k_nope_norm_g, mla_k_rope_norm_g, dsa_q_norm_g, dsa_k_norm_g, cd_w_out, ffn_norm_g,
           router_group_w, router_group_b, router_expert_w, router_expert_b, expert_w_gate, expert_w_up,
           expert_w_down):
    batch, seq, d = x.shape
    topk = min(INDEX_TOPK, seq // 4)
    depth = ffn_norm_g.shape[0]
    h = x.reshape(batch * seq, d)
    for layer in range(depth):
        i = layer // 2
        if layer % 2 == 0:
            h = _even_mixer(h, batch, seq, ab_norm_g[i], ab_w_in[i], sb_q_norm_g[i], sb_k_norm_g[i],
                            sgu_norm_g[i], sgu_w_s[i], sgu_b_s[i], ab_w_out[i])
        else:
            h = _odd_mixer(h, positions, batch, seq, cd_norm_g[i], cd_w_in[i], mla_q_latent_norm_g[i],
                           mla_kv_latent_norm_g[i], mla_w_uq[i], mla_w_ukv[i], mla_q_norm_g[i],
                           mla_k_nope_norm_g[i], mla_k_rope_norm_g[i], dsa_q_norm_g[i], dsa_k_norm_g[i],
                           cd_w_out[i], topk)
        h = _moe(h, ffn_norm_g[layer], router_group_w[layer], router_group_b[layer], router_expert_w[layer],
                 router_expert_b[layer], expert_w_gate[layer], expert_w_up[layer], expert_w_down[layer])
    return h.reshape(batch, seq, d)
```

```python
import functools

import numpy as np
import jax
import jax.numpy as jnp
from jax import lax
from jax.experimental import pallas as pl
from jax.experimental.pallas import tpu as pltpu

F32 = jnp.float32
BF16 = jnp.bfloat16

LANES = 128
VMEM_LIMIT = 48 * 1024 * 1024

D_MODEL = 1024
CHUNK = 64
ROPE_THETA = 500000.0
NORM_EPS = 1e-6

SB_HEADS = 8
HEAD_DIM = 64
SGU_CHUNK = 128

MLA_HEADS = 8
MLA_Q_RANK = 256
MLA_KV_RANK = 128
MLA_ROPE_DIM = 32
MLA_NOPE_DIM = 64
DSA_ROT_DIM = 16
IDX_HEADS = 8
IDX_HEAD_DIM = 32
IDX_ROT_DIM = 8
INDEX_TOPK = 256

MOE_GROUPS = 4
EXPERTS_PER_GROUP = 4
N_EXPERTS = 16
D_EXPERT = 512

ATT_BLOCK = 512
QUERY_TILE = 256
SB_BLOCK = 512
KEY_BLOCK = 512
SUB_BLOCK = 128
ROW_TILE = 256
MOE_TILE = 512
MASK_VALUE = -1e30
LOG2_E = 1.4426950408889634
INT_MIN = -2147483648


def _params(*semantics):
    return pltpu.CompilerParams(dimension_semantics=semantics, vmem_limit_bytes=VMEM_LIMIT)


def _lane_iota(shape=(1, LANES)):
    return lax.broadcasted_iota(jnp.int32, shape, len(shape) - 1)


def _rms(x, g, n):
    ms = jnp.sum(x * x, axis=-1, keepdims=True) * (1.0 / n)
    return x * lax.rsqrt(ms + NORM_EPS) * g


def _pair_rms(x, g):
    in_a = _lane_iota() < HEAD_DIM
    ss = x * x
    sa = jnp.sum(jnp.where(in_a, ss, 0.0), axis=-1, keepdims=True)
    sb = jnp.sum(jnp.where(in_a, 0.0, ss), axis=-1, keepdims=True)
    ms = jnp.where(in_a, sa, sb) * (1.0 / HEAD_DIM)
    return x * lax.rsqrt(ms + NORM_EPS) * g


def _rotate(x, tab_ref, half):
    up = pltpu.roll(x, LANES - half, 1)
    down = pltpu.roll(x, half, 1)
    return x * tab_ref[0] + up * tab_ref[1] + down * tab_ref[2]


def _dot_t(a, b):
    return lax.dot_general(a, b, (((1,), (1,)), ((), ())), preferred_element_type=F32)


def _norm_matmul_kernel(x_ref, g_ref, w_ref, o_ref):
    xb = _rms(x_ref[...], g_ref[...], D_MODEL).astype(BF16)
    n_blocks = o_ref.shape[0]
    step = 4
    for c in range(0, n_blocks, step):
        nb = min(step, n_blocks - c)
        acc = jnp.dot(xb, w_ref[:, c * LANES:(c + nb) * LANES], preferred_element_type=F32)
        for i in range(nb):
            o_ref[c + i] = acc[:, i * LANES:(i + 1) * LANES]


def _norm_matmul(h, g, w):
    t, d = h.shape
    n = w.shape[1]
    nb = n // LANES
    return pl.pallas_call(
        _norm_matmul_kernel,
        grid=(t // ROW_TILE,),
        in_specs=[pl.BlockSpec((ROW_TILE, d), lambda i: (i, 0)),
                  pl.BlockSpec((1, d), lambda i: (0, 0)),
                  pl.BlockSpec((d, n), lambda i: (0, 0))],
        out_specs=pl.BlockSpec((nb, ROW_TILE, LANES), lambda i: (0, i, 0)),
        out_shape=jax.ShapeDtypeStruct((nb, t, LANES), F32),
        compiler_params=_params("parallel"),
        name="norm_matmul",
    )(h, g.reshape(1, d), w)


def _out_proj_kernel(h_ref, a_ref, b_ref, w1_ref, w2_ref, o_ref):
    acc = jnp.dot(a_ref[...], w1_ref[...], preferred_element_type=F32)
    acc = acc + jnp.dot(b_ref[...], w2_ref[...], preferred_element_type=F32)
    o_ref[...] = h_ref[...] + acc


def _out_proj(h, a, b, w):
    t, d = h.shape
    ka, kb = a.shape[1], b.shape[1]
    return pl.pallas_call(
        _out_proj_kernel,
        grid=(t // ROW_TILE,),
        in_specs=[pl.BlockSpec((ROW_TILE, d), lambda i: (i, 0)),
                  pl.BlockSpec((ROW_TILE, ka), lambda i: (i, 0)),
                  pl.BlockSpec((ROW_TILE, kb), lambda i: (i, 0)),
                  pl.BlockSpec((ka, d), lambda i: (0, 0)),
                  pl.BlockSpec((kb, d), lambda i: (0, 0))],
        out_specs=pl.BlockSpec((ROW_TILE, d), lambda i: (i, 0)),
        out_shape=jax.ShapeDtypeStruct((t, d), F32),
        compiler_params=_params("parallel"),
        name="out_proj",
    )(h, a, b, w[:ka].astype(BF16), w[ka:].astype(BF16))


def _sb_kernel(q_ref, k_ref, v_ref, gq_ref, gk_ref, o_ref, kn_ref, vt_ref, *, seq):
    tq, tk, sub = SB_BLOCK, KEY_BLOCK, SUB_BLOCK
    n_q = tq // sub
    i = pl.program_id(2)
    in_a = _lane_iota() < HEAD_DIM

    @pl.when(i == 0)
    def _():
        def prep(r, carry):
            rows = pl.ds(pl.multiple_of(r * sub, sub), sub)
            kn_ref[rows, :] = _pair_rms(k_ref[0, rows, :], gk_ref[...]).astype(BF16)
            vt_ref[r] = v_ref[0, rows, :].T.astype(BF16)
            return carry
        lax.fori_loop(0, seq // sub, prep, 0)

    qn = _pair_rms(q_ref[0], gq_ref[...]) * (HEAD_DIM ** -0.5 * LOG2_E)
    qa = jnp.where(in_a, qn, 0.0).astype(BF16)
    qb = jnp.where(in_a, 0.0, qn).astype(BF16)
    q_cat = [jnp.concatenate([qa[g * sub:(g + 1) * sub], qb[g * sub:(g + 1) * sub]], axis=0) for g in range(n_q)]

    r_i = lax.broadcasted_iota(jnp.int32, (sub, sub), 0)
    c_i = lax.broadcasted_iota(jnp.int32, (sub, sub), 1)
    after_eye = jnp.concatenate([jnp.where(c_i > r_i, -1.0, 0.0), jnp.where(c_i == r_i, 1.0, 0.0)],
                                axis=1).astype(BF16)
    key_l = lax.broadcasted_iota(jnp.int32, (sub, 2 * sub), 0)
    qry_l = lax.broadcasted_iota(jnp.int32, (sub, 2 * sub), 1) % sub
    strict_local = key_l < qry_l

    def block(sub_base, n_sub, carry, diag):
        tiles = [(g, kh) for g in range(n_q) for kh in reversed(range(n_sub)) if not (diag and kh > g)]
        masks = {t: (strict_local if (diag and t[0] == t[1]) else None) for t in tiles}
        k_sub = [kn_ref[pl.ds(pl.multiple_of((sub_base + kh) * sub, sub), sub), :] for kh in range(n_sub)]
        z = {t: _dot_t(k_sub[t[1]], q_cat[t[0]]) for t in tiles}
        r, col_sum = {}, {}
        for t in tiles:
            sp = jnp.maximum(z[t], 0.0) + jnp.log2(1.0 + jnp.exp2(-jnp.abs(z[t])))
            log_sig = z[t] - sp
            if masks[t] is not None:
                sp = jnp.where(masks[t], sp, 0.0)
            col_sum[t] = jnp.sum(sp, axis=0, keepdims=True)
            rhs = jnp.concatenate([sp.astype(BF16), log_sig.astype(BF16)], axis=0)
            r[t] = jnp.dot(after_eye, rhs, preferred_element_type=F32)
        out = []
        for g in range(n_q):
            c, acc_a, acc_b = carry[g]
            for kh in reversed(range(n_sub)):
                t = (g, kh)
                if t not in r:
                    continue
                w = jnp.exp2(r[t] + c)
                if masks[t] is not None:
                    w = jnp.where(masks[t], w, 0.0)
                c = c - col_sum[t]
                o = jnp.dot(vt_ref[sub_base + kh], w.astype(BF16), preferred_element_type=F32)
                acc_a = acc_a + o[:HEAD_DIM, :sub]
                acc_b = acc_b + o[HEAD_DIM:, sub:]
            out.append((c, acc_a, acc_b))
        return tuple(out)

    init = (jnp.zeros((1, 2 * sub), F32), jnp.zeros((HEAD_DIM, sub), F32), jnp.zeros((HEAD_DIM, sub), F32))
    carry = block(i * n_q, n_q, (init,) * n_q, True)
    per_step = tk // sub
    carry = lax.fori_loop(0, i * (tq // tk),
                          lambda n, c: block(i * n_q - (n + 1) * per_step, per_step, c, False), carry)
    for g in range(n_q):
        _, acc_a, acc_b = carry[g]
        o_ref[g * sub:(g + 1) * sub, :] = jnp.concatenate([acc_a, acc_b], axis=0).T.astype(o_ref.dtype)


def _sb_attention(proj, gq, gk, batch, seq):
    t = batch * seq
    nq = seq // SB_BLOCK
    pairs = SB_HEADS // 2
    return pl.pallas_call(
        functools.partial(_sb_kernel, seq=seq),
        grid=(batch, pairs, nq),
        in_specs=[pl.BlockSpec((1, SB_BLOCK, LANES), lambda b, p, i: (p, b * nq + i, 0)),
                  pl.BlockSpec((1, seq, LANES), lambda b, p, i: (pairs + p, b, 0)),
                  pl.BlockSpec((1, seq, LANES), lambda b, p, i: (2 * pairs + p, b, 0)),
                  pl.BlockSpec((1, LANES), lambda b, p, i: (0, 0)),
                  pl.BlockSpec((1, LANES), lambda b, p, i: (0, 0))],
        out_specs=pl.BlockSpec((SB_BLOCK, LANES), lambda b, p, i: (b * nq + i, p)),
        out_shape=jax.ShapeDtypeStruct((t, pairs * LANES), BF16),
        scratch_shapes=[pltpu.VMEM((seq, LANES), BF16), pltpu.VMEM((seq // SUB_BLOCK, LANES, SUB_BLOCK), BF16)],
        compiler_params=_params("parallel", "parallel", "arbitrary"),
        name="stick_breaking",
    )(proj, proj, proj, jnp.tile(gq, 2).reshape(1, LANES), jnp.tile(gk, 2).reshape(1, LANES))


def _sgu_kernel(u_ref, z_ref, g_ref, w_ref, b_ref, o_ref):
    in_a = _lane_iota() < HEAD_DIM
    zn = _pair_rms(jax.nn.gelu(z_ref[0]), g_ref[0]).astype(BF16)
    u = jax.nn.gelu(u_ref[0])
    row = lax.broadcasted_iota(jnp.int32, (SGU_CHUNK, SGU_CHUNK), 0) // CHUNK
    col = lax.broadcasted_iota(jnp.int32, (SGU_CHUNK, SGU_CHUNK), 1) // CHUNK
    visible = col <= row
    wa = jnp.where(visible, w_ref[0], 0.0).astype(BF16)
    wb = jnp.where(visible, w_ref[1], 0.0).astype(BF16)
    for n in range(u.shape[0] // SGU_CHUNK):
        rows = slice(n * SGU_CHUNK, (n + 1) * SGU_CHUNK)
        zc = zn[rows]
        mixed = jnp.where(in_a, jnp.dot(wa, zc, preferred_element_type=F32),
                          jnp.dot(wb, zc, preferred_element_type=F32)) + b_ref[0]
        o_ref[rows, :] = (u[rows] * mixed).astype(o_ref.dtype)


def _spatial_gating(proj, sgu_g, w_s, b_s):
    t = proj.shape[1]
    pairs = w_s.shape[0] // 2
    g = sgu_g.reshape(pairs, 1, LANES)
    bias = jnp.repeat(b_s.reshape(pairs, 2, SGU_CHUNK), HEAD_DIM, axis=1).transpose(0, 2, 1)
    return pl.pallas_call(
        _sgu_kernel,
        grid=(t // ROW_TILE, pairs),
        in_specs=[pl.BlockSpec((1, ROW_TILE, LANES), lambda i, p: (12 + p, i, 0)),
                  pl.BlockSpec((1, ROW_TILE, LANES), lambda i, p: (16 + p, i, 0)),
                  pl.BlockSpec((1, 1, LANES), lambda i, p: (p, 0, 0)),
                  pl.BlockSpec((2, SGU_CHUNK, SGU_CHUNK), lambda i, p: (p, 0, 0)),
                  pl.BlockSpec((1, SGU_CHUNK, LANES), lambda i, p: (p, 0, 0))],
        out_specs=pl.BlockSpec((ROW_TILE, LANES), lambda i, p: (i, p)),
        out_shape=jax.ShapeDtypeStruct((t, pairs * LANES), BF16),
        compiler_params=_params("parallel", "parallel"),
        name="spatial_gating",
    )(proj, proj, g, w_s, bias)


def _moe_kernel(h_ref, g_ref, wr_ref, br_ref, wgu_ref, wd_ref, o_ref, xn_ref, gates_ref):
    e = pl.program_id(1)
    lane = _lane_iota()
    first_expert_lane = MOE_GROUPS

    @pl.when(e == 0)
    def _():
        x = h_ref[...]
        xn = _rms(x, g_ref[...], D_MODEL)
        xn_ref[...] = xn.astype(BF16)
        logits = jnp.dot(xn, wr_ref[...], preferred_element_type=F32,
                         precision=lax.Precision.HIGHEST) + br_ref[...]
        neg = -jnp.inf
        is_group = lane < MOE_GROUPS
        mg = jnp.max(jnp.where(is_group, logits, neg), axis=-1, keepdims=True)
        sg = jnp.sum(jnp.exp(jnp.where(is_group, logits - mg, neg)), axis=-1, keepdims=True)
        p_group = 1.0 / sg
        g_idx = jnp.min(jnp.where(is_group & (logits == mg), lane, LANES), axis=-1, keepdims=True)
        lo = first_expert_lane + EXPERTS_PER_GROUP * g_idx
        in_group = (lane >= lo) & (lane < lo + EXPERTS_PER_GROUP)
        m1 = jnp.max(jnp.where(in_group, logits, neg), axis=-1, keepdims=True)
        e1 = jnp.min(jnp.where(in_group & (logits == m1), lane, LANES), axis=-1, keepdims=True)
        rest = in_group & (lane != e1)
        m2 = jnp.max(jnp.where(rest, logits, neg), axis=-1, keepdims=True)
        e2 = jnp.min(jnp.where(rest & (logits == m2), lane, LANES), axis=-1, keepdims=True)
        r = jnp.exp(m2 - m1)
        c1 = p_group / (1.0 + r)
        gates_ref[...] = jnp.where(lane == e1, c1, 0.0) + jnp.where(lane == e2, c1 * r, 0.0)
        o_ref[...] = x

    gate = jnp.sum(jnp.where(lane == e + first_expert_lane, gates_ref[...], 0.0), axis=-1, keepdims=True)
    gu = jnp.dot(xn_ref[...], wgu_ref[0], preferred_element_type=F32)
    hid = jax.nn.silu(gu[:, :D_EXPERT]) * gu[:, D_EXPERT:] * gate
    o_ref[...] += jnp.dot(hid.astype(BF16), wd_ref[0], preferred_element_type=F32)


def _moe(h, g, w_group, b_group, w_expert, b_expert, w_gate, w_up, w_down):
    t, d = h.shape
    pad = LANES - MOE_GROUPS - N_EXPERTS
    wr = jnp.concatenate([w_group, w_expert.transpose(1, 0, 2).reshape(d, N_EXPERTS),
                          jnp.zeros((d, pad), F32)], axis=1)
    br = jnp.concatenate([b_group, b_expert.reshape(N_EXPERTS), jnp.zeros((pad,), F32)]).reshape(1, LANES)
    wgu = jnp.concatenate([w_gate, w_up], axis=2).astype(BF16)
    return pl.pallas_call(
        _moe_kernel,
        grid=(t // MOE_TILE, N_EXPERTS),
        in_specs=[pl.BlockSpec((MOE_TILE, d), lambda i, e: (i, 0)),
                  pl.BlockSpec((1, d), lambda i, e: (0, 0)),
                  pl.BlockSpec((d, LANES), lambda i, e: (0, 0)),
                  pl.BlockSpec((1, LANES), lambda i, e: (0, 0)),
                  pl.BlockSpec((1, d, 2 * D_EXPERT), lambda i, e: (e, 0, 0)),
                  pl.BlockSpec((1, D_EXPERT, d), lambda i, e: (e, 0, 0))],
        out_specs=pl.BlockSpec((MOE_TILE, d), lambda i, e: (i, 0)),
        out_shape=jax.ShapeDtypeStruct((t, d), F32),
        scratch_shapes=[pltpu.VMEM((MOE_TILE, d), BF16), pltpu.VMEM((MOE_TILE, LANES), F32)],
        compiler_params=_params("parallel", "arbitrary"),
        name="moe",
    )(h, g.reshape(1, d), wr, br, wgu, w_down.astype(BF16))


CD_CQ, CD_CKV, CD_QD, CD_KD, CD_VD, CD_IQ, CD_IK, CD_MISC = 0, 2, 3, 7, 11, 15, 17, 18
CD_BLOCKS = 19
KPE_LANE = 64


def _cd_prep_kernel(p_ref, rot_mla_ref, rot_dsa_ref, rot_idx_ref, gql_ref, gkvl_ref, wuq_ref, wk_ref,
                    wv_ref, gq_ref, gkn_ref, gkr_ref, gdq_ref, gdk_ref,
                    qc_ref, kc_ref, vc_ref, qd_ref, kd_ref, vd_ref, iq_ref, ik_ref, iw_ref):
    lane = _lane_iota()
    head_w = MLA_ROPE_DIM + MLA_NOPE_DIM
    cq = jnp.concatenate([p_ref[CD_CQ], p_ref[CD_CQ + 1]], axis=1)
    cqn = _rms(cq, gql_ref[...], MLA_Q_RANK).astype(BF16)
    ckvn = _rms(p_ref[CD_CKV], gkvl_ref[...], MLA_KV_RANK).astype(BF16)
    misc = p_ref[CD_MISC]
    kpe = jnp.where((lane >= KPE_LANE) & (lane < KPE_LANE + MLA_ROPE_DIM), misc, 0.0)
    k_rope = _rotate(_rms(kpe, gkr_ref[...], MLA_ROPE_DIM), rot_mla_ref, MLA_ROPE_DIM // 2)
    for h in range(MLA_HEADS):
        cols = slice(h * LANES, (h + 1) * LANES)
        q = jnp.dot(cqn, wuq_ref[:, cols], preferred_element_type=F32)
        q = _rotate(_rms(q, gq_ref[...], head_w), rot_mla_ref, MLA_ROPE_DIM // 2)
        qc_ref[h] = (q * (head_w ** -0.5 * LOG2_E)).astype(BF16)
        kn = jnp.dot(ckvn, wk_ref[:, cols], preferred_element_type=F32)
        kc_ref[h] = (_rms(kn, gkn_ref[...], MLA_NOPE_DIM) + k_rope).astype(BF16)
    vc = jnp.dot(ckvn, wv_ref[...], preferred_element_type=F32)
    vd = jnp.concatenate([p_ref[CD_VD + p] for p in range(SB_HEADS // 2)], axis=1)
    for r in range(vc_ref.shape[0]):
        rows = slice(r * SUB_BLOCK, (r + 1) * SUB_BLOCK)
        vc_ref[r] = vc[rows].T.astype(BF16)
        vd_ref[r] = vd[rows].T.astype(BF16)
    for p in range(SB_HEADS // 2):
        q = _rotate(_pair_rms(p_ref[CD_QD + p], gdq_ref[...]), rot_dsa_ref, DSA_ROT_DIM // 2)
        qd_ref[p] = (q * (HEAD_DIM ** -0.5 * LOG2_E)).astype(BF16)
        k = _rotate(_pair_rms(p_ref[CD_KD + p], gdk_ref[...]), rot_dsa_ref, DSA_ROT_DIM // 2)
        kd_ref[p] = k.astype(BF16)
    for p in range(2):
        iq = _rotate(p_ref[CD_IQ + p], rot_idx_ref, IDX_ROT_DIM // 2)
        iq_ref[p] = (iq * (IDX_HEAD_DIM ** -0.5)).astype(BF16)
    ik_ref[...] = _rotate(p_ref[CD_IK], rot_idx_ref, IDX_ROT_DIM // 2).astype(BF16)
    iw_ref[...] = misc * (IDX_HEADS ** -0.5)


def _rot_tables(positions, rot_dim, group_w, offset):
    half = rot_dim // 2
    inv_freq = jnp.power(jnp.float32(ROPE_THETA), -jnp.arange(half, dtype=F32) * (2.0 / rot_dim))
    ang = positions.reshape(-1).astype(F32)[:, None] * inv_freq
    t = ang.shape[0]
    cos = jnp.concatenate([jnp.cos(ang), jnp.ones((t, 1), F32)], axis=1)
    sin = jnp.concatenate([jnp.sin(ang), jnp.zeros((t, 1), F32)], axis=1)
    p = np.arange(LANES) % group_w - offset
    first = (p >= 0) & (p < half)
    second = (p >= half) & (p < rot_dim)
    idx_c = np.where(first, p, np.where(second, p - half, half))
    idx_1 = np.where(first, p, half)
    idx_2 = np.where(second, p - half, half)
    return jnp.stack([cos[:, idx_c], -sin[:, idx_1], sin[:, idx_2]])


def _cd_prep(proj, positions, q_lat_g, kv_lat_g, w_uq, w_ukv, mla_q_g, mla_kn_g, mla_kr_g, dsa_q_g, dsa_k_g):
    t = proj.shape[1]
    head_w = MLA_ROPE_DIM + MLA_NOPE_DIM
    zpad = LANES - head_w
    wuq = w_uq.reshape(MLA_Q_RANK, MLA_HEADS, head_w)
    wuq = jnp.concatenate([wuq[..., MLA_ROPE_DIM:], wuq[..., :MLA_ROPE_DIM],
                           jnp.zeros((MLA_Q_RANK, MLA_HEADS, zpad), F32)], axis=-1)
    wuq = wuq.reshape(MLA_Q_RANK, MLA_HEADS * LANES).astype(BF16)
    wukv = w_ukv.reshape(MLA_KV_RANK, MLA_HEADS, MLA_NOPE_DIM + HEAD_DIM)
    wk = jnp.concatenate([wukv[..., :MLA_NOPE_DIM], jnp.zeros((MLA_KV_RANK, MLA_HEADS, LANES - MLA_NOPE_DIM), F32)],
                         axis=-1).reshape(MLA_KV_RANK, MLA_HEADS * LANES).astype(BF16)
    wv = wukv[..., MLA_NOPE_DIM:].reshape(MLA_KV_RANK, MLA_HEADS * HEAD_DIM).astype(BF16)
    gq = jnp.concatenate([mla_q_g[MLA_ROPE_DIM:], mla_q_g[:MLA_ROPE_DIM], jnp.zeros((zpad,), F32)])
    gkn = jnp.concatenate([mla_kn_g, jnp.zeros((LANES - MLA_NOPE_DIM,), F32)])
    gkr = jnp.concatenate([jnp.zeros((KPE_LANE,), F32), mla_kr_g,
                           jnp.zeros((LANES - KPE_LANE - MLA_ROPE_DIM,), F32)])
    rot_mla = _rot_tables(positions, MLA_ROPE_DIM, LANES, MLA_NOPE_DIM)
    rot_dsa = _rot_tables(positions, DSA_ROT_DIM, HEAD_DIM, 0)
    rot_idx = _rot_tables(positions, IDX_ROT_DIM, IDX_HEAD_DIM, 0)

    tm = ROW_TILE
    row = lambda i: (i, 0)
    const2 = lambda i: (0, 0)
    blk3 = lambda i: (0, i, 0)
    vec = pl.BlockSpec((1, LANES), const2)
    bf_blocks = lambda n: jax.ShapeDtypeStruct((n, t, LANES), BF16)
    blk_out = lambda n: pl.BlockSpec((n, tm, LANES), blk3)
    v_width = SB_HEADS * HEAD_DIM
    vt_shape = jax.ShapeDtypeStruct((t // SUB_BLOCK, v_width, SUB_BLOCK), BF16)
    vt_out = pl.BlockSpec((tm // SUB_BLOCK, v_width, SUB_BLOCK), lambda i: (i, 0, 0))
    return pl.pallas_call(
        _cd_prep_kernel,
        grid=(t // tm,),
        in_specs=[pl.BlockSpec((CD_BLOCKS, tm, LANES), blk3),
                  pl.BlockSpec((3, tm, LANES), blk3),
                  pl.BlockSpec((3, tm, LANES), blk3),
                  pl.BlockSpec((3, tm, LANES), blk3),
                  pl.BlockSpec((1, MLA_Q_RANK), const2),
                  vec,
                  pl.BlockSpec(wuq.shape, const2),
                  pl.BlockSpec(wk.shape, const2),
                  pl.BlockSpec(wv.shape, const2),
                  vec, vec, vec, vec, vec],
        out_specs=[blk_out(8), blk_out(8), vt_out, blk_out(4), blk_out(4), vt_out, blk_out(2),
                   pl.BlockSpec((tm, LANES), row), pl.BlockSpec((tm, LANES), row)],
        out_shape=[bf_blocks(8), bf_blocks(8), vt_shape, bf_blocks(4), bf_blocks(4), vt_shape,
                   bf_blocks(2), jax.ShapeDtypeStruct((t, LANES), BF16), jax.ShapeDtypeStruct((t, LANES), F32)],
        compiler_params=_params("parallel"),
        name="cd_prep",
    )(proj, rot_mla, rot_dsa, rot_idx, q_lat_g.reshape(1, -1), kv_lat_g.reshape(1, -1), wuq, wk, wv,
      gq.reshape(1, -1), gkn.reshape(1, -1), gkr.reshape(1, -1),
      jnp.tile(dsa_q_g, 2).reshape(1, -1), jnp.tile(dsa_k_g, 2).reshape(1, -1))


def _softmax_update(s_tiles, vt_tiles, state):
    m, l, acc = state
    m_new = m
    for s in s_tiles:
        m_new = jnp.maximum(m_new, jnp.max(s, axis=0, keepdims=True))
    alpha = jnp.exp2(m - m_new)
    l = alpha * l
    acc = alpha * acc
    for s, vt in zip(s_tiles, vt_tiles):
        p = jnp.exp2(s - m_new)
        l = l + jnp.sum(p, axis=0, keepdims=True)
        acc = acc + jnp.dot(vt, p.astype(BF16), preferred_element_type=F32)
    return m_new, l, acc


def _softmax_init():
    return (jnp.full((1, QUERY_TILE), MASK_VALUE, F32), jnp.zeros((1, QUERY_TILE), F32),
            jnp.zeros((HEAD_DIM, QUERY_TILE), F32))


def _diag_visibility(kh, g):
    first_key_chunk, last_key_chunk = kh * SUB_BLOCK // CHUNK, ((kh + 1) * SUB_BLOCK - 1) // CHUNK
    first_q_chunk, last_q_chunk = g * QUERY_TILE // CHUNK, ((g + 1) * QUERY_TILE - 1) // CHUNK
    if first_key_chunk > last_q_chunk:
        return "none"
    if last_key_chunk <= first_q_chunk:
        return "all"
    key_c = lax.broadcasted_iota(jnp.int32, (SUB_BLOCK, QUERY_TILE), 0) // CHUNK + first_key_chunk
    qry_c = lax.broadcasted_iota(jnp.int32, (SUB_BLOCK, QUERY_TILE), 1) // CHUNK + first_q_chunk
    return key_c <= qry_c


def _finish_pair(states):
    outs = []
    for g in range(len(states[0])):
        both = jnp.concatenate([states[h][g][2] / states[h][g][1] for h in range(2)], axis=0)
        outs.append(both.T)
    return outs


def _attend_block(score_fn, vt_fn, n_sub, states, diag):
    n_g = len(states[0])
    scores = {}
    for h in range(2):
        for g in range(n_g):
            for kh in range(n_sub):
                vis = _diag_visibility(kh, g) if diag else "all"
                if isinstance(vis, str) and vis == "none":
                    continue
                scores[h, g, kh] = score_fn(h, g, kh, vis)
    out = []
    for h in range(2):
        per_g = []
        for g in range(n_g):
            khs = [kh for kh in range(n_sub) if (h, g, kh) in scores]
            per_g.append(_softmax_update([scores[h, g, kh] for kh in khs], [vt_fn(h, kh) for kh in khs],
                                         states[h][g]))
        out.append(tuple(per_g))
    return tuple(out)


def _attend(score_fn, vt_fn, i, o_ref):
    tq, tk, sub, qw = ATT_BLOCK, KEY_BLOCK, SUB_BLOCK, QUERY_TILE
    per_step = tk // sub
    states = ((_softmax_init(),) * (tq // qw),) * 2

    def step(n, st):
        base = n * per_step
        return _attend_block(lambda h, g, kh, vis: score_fn(h, g, base + kh, vis),
                             lambda h, kh: vt_fn(h, base + kh), per_step, st, False)

    states = lax.fori_loop(0, i * (tq // tk), step, states)
    base = i * (tq // sub)
    states = _attend_block(lambda h, g, kh, vis: score_fn(h, g, base + kh, vis),
                           lambda h, kh: vt_fn(h, base + kh), tq // sub, states, True)
    for g, tile in enumerate(_finish_pair(states)):
        o_ref[g * qw:(g + 1) * qw, :] = tile.astype(o_ref.dtype)


def _mla_kernel(q_ref, k_ref, vt_ref, o_ref):
    sub, qw = SUB_BLOCK, QUERY_TILE

    def score(h, g, sub_idx, vis):
        rows = pl.ds(pl.multiple_of(sub_idx * sub, sub), sub)
        s = _dot_t(k_ref[h, rows, :], q_ref[h, g * qw:(g + 1) * qw, :])
        return s if isinstance(vis, str) else jnp.where(vis, s, MASK_VALUE)

    _attend(score, lambda h, sub_idx: vt_ref[sub_idx, h * HEAD_DIM:(h + 1) * HEAD_DIM, :],
            pl.program_id(2), o_ref)


def _mla_attention(qc, kc, vct, batch, seq):
    t = batch * seq
    nq = seq // ATT_BLOCK
    pairs = MLA_HEADS // 2
    return pl.pallas_call(
        _mla_kernel,
        grid=(batch, pairs, nq),
        in_specs=[pl.BlockSpec((2, ATT_BLOCK, LANES), lambda b, p, i: (p, b * nq + i, 0)),
                  pl.BlockSpec((2, seq, LANES), lambda b, p, i: (p, b, 0)),
                  pl.BlockSpec((seq // SUB_BLOCK, LANES, SUB_BLOCK), lambda b, p, i: (b, p, 0))],
        out_specs=pl.BlockSpec((ATT_BLOCK, LANES), lambda b, p, i: (b * nq + i, p)),
        out_shape=jax.ShapeDtypeStruct((t, pairs * LANES), BF16),
        compiler_params=_params("parallel", "parallel", "arbitrary"),
        name="latent_attention",
    )(qc, kc, vct)


def _fold_rows(x):
    parts = [x[r * 8:(r + 1) * 8] for r in range(x.shape[0] // 8)]
    while len(parts) > 1:
        parts = [parts[k] + parts[k + 1] for k in range(0, len(parts), 2)]
    return parts[0]


def _dsa_select(iq_ref, ik_ref, iw_ref, key_ref, iqm_ref, iwt_ref, i, topk):
    tq, tk, sub, qw = ATT_BLOCK, KEY_BLOCK, SUB_BLOCK, QUERY_TILE
    n_g = tq // qw
    per_step = tk // sub
    n_steps = (i + 1) * (tq // tk)
    lane = _lane_iota()
    heads_per_blk = LANES // IDX_HEAD_DIM
    for h in range(IDX_HEADS):
        mine = lane // IDX_HEAD_DIM == h % heads_per_blk
        iqm_ref[h] = jnp.where(mine, iq_ref[h // heads_per_blk], 0.0).astype(BF16)
    iwt_ref[...] = iw_ref[...].T[:IDX_HEADS]

    def score_keys(sub_idx, g):
        ik = ik_ref[pl.ds(pl.multiple_of(sub_idx * sub, sub), sub), :]
        cols = slice(g * qw, (g + 1) * qw)
        logits = [_dot_t(ik, iqm_ref[h, cols, :]) for h in range(IDX_HEADS)]
        score = iwt_ref[0:1, cols] * jnp.maximum(logits[0], 0.0)
        for h in range(1, IDX_HEADS):
            score = score + iwt_ref[h:h + 1, cols] * jnp.maximum(logits[h], 0.0)
        bits = pltpu.bitcast(score, jnp.int32)
        return bits ^ ((bits >> 31) & 0x7FFFFFFF)

    def score_step(n, carry):
        for kh in range(per_step):
            for g in range(n_g):
                key_ref[n * per_step + kh, g] = score_keys(n * per_step + kh, g)
        return carry

    lax.fori_loop(0, i * (tq // tk), score_step, 0)
    diag_base = i * (tq // sub)
    for kh in range(tq // sub):
        for g in range(n_g):
            vis = _diag_visibility(kh, g)
            if isinstance(vis, str) and vis == "none":
                key_ref[diag_base + kh, g] = jnp.full((sub, qw), INT_MIN, jnp.int32)
            else:
                keys = score_keys(diag_base + kh, g)
                key_ref[diag_base + kh, g] = keys if isinstance(vis, str) else jnp.where(vis, keys, INT_MIN)

    def count(pred):
        def body(n, cnts):
            out = []
            for g in range(n_g):
                c = cnts[g]
                for kh in range(per_step):
                    c = c + _fold_rows(jnp.where(pred(key_ref[n * per_step + kh, g], g), 1, 0))
                out.append(c)
            return tuple(out)
        cnts = lax.fori_loop(0, n_steps, body, (jnp.zeros((8, qw), jnp.int32),) * n_g)
        return [jnp.sum(c, axis=0, keepdims=True) for c in cnts]

    def bit_body(b, prefix):
        bit = lax.shift_left(jnp.int32(1), 31 - b)
        cands = [pf | bit for pf in prefix]
        cand_keys = [c ^ INT_MIN for c in cands]
        cnt = count(lambda key, g: key >= cand_keys[g])
        return tuple(jnp.where(cnt[g] >= topk, cands[g], prefix[g]) for g in range(n_g))

    zero = (jnp.zeros((1, qw), jnp.int32),) * n_g
    prefix = lax.cond((i + 1) * tq > topk, lambda: lax.fori_loop(0, 32, bit_body, zero), lambda: zero)
    kth = [pf ^ INT_MIN for pf in prefix]
    n_gt = count(lambda key, g: key > kth[g])
    n_ties = [(topk - c).astype(F32) for c in n_gt]

    s_i = lax.broadcasted_iota(jnp.int32, (sub, sub), 0)
    s_j = lax.broadcasted_iota(jnp.int32, (sub, sub), 1)
    before = jnp.where(s_j < s_i, 1.0, 0.0).astype(BF16)

    def mask_step(n, seen):
        out = []
        for g in range(n_g):
            sg = seen[g]
            for kh in range(per_step):
                key = key_ref[n * per_step + kh, g]
                tie = key == kth[g]
                tie_f = jnp.where(tie, 1.0, 0.0)
                rank = jnp.dot(before, tie_f.astype(BF16), preferred_element_type=F32) + sg
                chosen = ((key > kth[g]) | (tie & (rank < n_ties[g]))) & (key != INT_MIN)
                bias = jnp.where(chosen, jnp.float32(0.0), jnp.float32(MASK_VALUE))
                key_ref[n * per_step + kh, g] = pltpu.bitcast(bias, jnp.int32)
                sg = sg + jnp.sum(tie_f, axis=0, keepdims=True)
            out.append(sg)
        return tuple(out)

    lax.fori_loop(0, n_steps, mask_step, (jnp.zeros((1, qw), F32),) * n_g)


def _dsa_kernel(qd_ref, kd_ref, vdt_ref, iq_ref, ik_ref, iw_ref, o_ref, key_ref, iqm_ref, iwt_ref, *, topk):
    sub, qw = SUB_BLOCK, QUERY_TILE
    i = pl.program_id(1)

    @pl.when(pl.program_id(2) == 0)
    def _():
        _dsa_select(iq_ref, ik_ref, iw_ref, key_ref, iqm_ref, iwt_ref, i, topk)

    in_a = _lane_iota() < HEAD_DIM
    q = qd_ref[0]
    q_heads = (jnp.where(in_a, q, 0.0).astype(BF16), jnp.where(in_a, 0.0, q).astype(BF16))

    def score(h, g, sub_idx, vis):
        rows = pl.ds(pl.multiple_of(sub_idx * sub, sub), sub)
        bias = pltpu.bitcast(key_ref[sub_idx, g], F32)
        return _dot_t(kd_ref[0, rows, :], q_heads[h][g * qw:(g + 1) * qw]) + bias

    _attend(score, lambda h, sub_idx: vdt_ref[sub_idx, h * HEAD_DIM:(h + 1) * HEAD_DIM, :], i, o_ref)


def _dsa_attention(qd, kd, vdt, iq, ik, iw, batch, seq, topk):
    t = batch * seq
    nq = seq // ATT_BLOCK
    pairs = SB_HEADS // 2
    n_sub = seq // SUB_BLOCK
    return pl.pallas_call(
        functools.partial(_dsa_kernel, topk=topk),
        grid=(batch, nq, pairs),
        in_specs=[pl.BlockSpec((1, ATT_BLOCK, LANES), lambda b, i, p: (p, b * nq + i, 0)),
                  pl.BlockSpec((1, seq, LANES), lambda b, i, p: (p, b, 0)),
                  pl.BlockSpec((n_sub, LANES, SUB_BLOCK), lambda b, i, p: (b, p, 0)),
                  pl.BlockSpec((2, ATT_BLOCK, LANES), lambda b, i, p: (0, b * nq + i, 0)),
                  pl.BlockSpec((seq, LANES), lambda b, i, p: (b, 0)),
                  pl.BlockSpec((ATT_BLOCK, LANES), lambda b, i, p: (b * nq + i, 0))],
        out_specs=pl.BlockSpec((ATT_BLOCK, LANES), lambda b, i, p: (b * nq + i, p)),
        out_shape=jax.ShapeDtypeStruct((t, pairs * LANES), BF16),
        scratch_shapes=[pltpu.VMEM((n_sub, ATT_BLOCK // QUERY_TILE, SUB_BLOCK, QUERY_TILE), jnp.int32),
                        pltpu.VMEM((IDX_HEADS, ATT_BLOCK, LANES), BF16),
                        pltpu.VMEM((IDX_HEADS, ATT_BLOCK), F32)],
        compiler_params=_params("parallel", "arbitrary", "arbitrary"),
        name="sparse_attention",
    )(qd, kd, vdt, iq, ik, iw)


def _even_mixer(h, batch, seq, norm_g, w_in, sb_q_g, sb_k_g, sgu_g, w_s, b_s, w_out):
    proj = _norm_matmul(h, norm_g, w_in.astype(BF16))
    a_out = _sb_attention(proj, sb_q_g, sb_k_g, batch, seq)
    b_out = _spatial_gating(proj, sgu_g, w_s, b_s)
    return _out_proj(h, a_out, b_out, w_out)


def _arrange_cd_w_in(w_in):
    d = w_in.shape[0]
    sizes = (MLA_Q_RANK, MLA_KV_RANK, MLA_ROPE_DIM, SB_HEADS * HEAD_DIM, SB_HEADS * HEAD_DIM,
             SB_HEADS * HEAD_DIM, IDX_HEADS * IDX_HEAD_DIM, IDX_HEAD_DIM, IDX_HEADS)
    c_q, c_kv, k_pe, q_d, k_d, v_d, iq, ik, iw = jnp.split(w_in, [int(c) for c in np.cumsum(sizes)[:-1]], axis=1)
    zeros = lambda n: jnp.zeros((d, n), w_in.dtype)
    misc = [iw, zeros(KPE_LANE - IDX_HEADS), k_pe, zeros(LANES - KPE_LANE - MLA_ROPE_DIM)]
    return jnp.concatenate([c_q, c_kv, q_d, k_d, v_d, iq, ik, ik, ik, ik] + misc, axis=1)


def _odd_mixer(h, positions, batch, seq, norm_g, w_in, q_lat_g, kv_lat_g, w_uq, w_ukv, mla_q_g, mla_kn_g,
               mla_kr_g, dsa_q_g, dsa_k_g, w_out, topk):
    proj = _norm_matmul(h, norm_g, _arrange_cd_w_in(w_in).astype(BF16))
    qc, kc, vc, qd, kd, vd, iq, ik, iw = _cd_prep(proj, positions, q_lat_g, kv_lat_g, w_uq, w_ukv, mla_q_g,
                                                  mla_kn_g, mla_kr_g, dsa_q_g, dsa_k_g)
    c_out = _mla_attention(qc, kc, vc, batch, seq)
    d_out = _dsa_attention(qd, kd, vd, iq, ik, iw, batch, seq, topk)
    return _out_proj(h, c_out, d_out, w_out)


def kernel(x, positions, ab_norm_g, ab_w_in, sb_q_norm_g, sb_k_norm_g, sgu_norm_g, sgu_w_s, sgu_b_s, ab_w_out,
           cd_norm_g, cd_w_in, mla_q_latent_norm_g, mla_kv_latent_norm_g, mla_w_uq, mla_w_ukv, mla_q_norm_g,
           mla_k_nope_norm_g, mla_k_rope_norm_g, dsa_q_norm_g, dsa_k_norm_g, cd_w_out, ffn_norm_g,
           router_group_w, router_group_b, router_expert_w, router_expert_b, expert_w_gate, expert_w_up,
           expert_w_down):
    batch, seq, d = x.shape
    topk = min(INDEX_TOPK, seq // 4)
    depth = ffn_norm_g.shape[0]
    h = x.reshape(batch * seq, d)
    for layer in range(depth):
        i = layer // 2
        if layer % 2 == 0:
            h = _even_mixer(h, batch, seq, ab_norm_g[i], ab_w_in[i], sb_q_norm_g[i], sb_k_norm_g[i],
                            sgu_norm_g[i], sgu_w_s[i], sgu_b_s[i], ab_w_out[i])
        else:
            h = _odd_mixer(h, positions, batch, seq, cd_norm_g[i], cd_w_in[i], mla_q_latent_norm_g[i],
                           mla_kv_latent_norm_g[i], mla_w_uq[i], mla_w_ukv[i], mla_q_norm_g[i],
                           mla_k_nope_norm_g[i], mla_k_rope_norm_g[i], dsa_q_norm_g[i], dsa_k_norm_g[i],
                           cd_w_out[i], topk)
        h = _moe(h, ffn_norm_g[layer], router_group_w[layer], router_group_b[layer], router_expert_w[layer],
                 router_expert_b[layer], expert_w_gate[layer], expert_w_up[layer], expert_w_down[layer])
    return h.reshape(batch, seq, d)
```
